```python
import math
import jax, jax.numpy as jnp
from jax import lax
import numpy as np

D_MODEL = 1024
BATCH = 2
SEQ = 8192
DEPTH = 2

CTX_LEN = 256
GRID_W = 64
Q_BLOCK = 128
RET_CHUNK = 128
ROPE_THETA = 10000.0
NORM_EPS = 1e-6
RWKV_LN_EPS = 64e-5

RET_HEADS = 4
RET_DK = 128
RET_DV = 128
GQA_Q_HEADS = 4
GQA_KV_HEADS = 2
GQA_GROUP = GQA_Q_HEADS // GQA_KV_HEADS
GQA_HEAD_DIM = 128
DIFF_HEADS = 4
DIFF_HEAD_DIM = 64
DIFF_V_DIM = 2 * DIFF_HEAD_DIM
RWKV_HEADS = 8
RWKV_HEAD_DIM = 64
RWKV_W = RWKV_HEADS * RWKV_HEAD_DIM
RWKV_W_LORA = 64
RWKV_A_LORA = 64
RWKV_G_LORA = 128
FFN_HIDDEN = 2816
N_EXPERTS = 8
TOP_K = 2
EXPERT_HIDDEN = 3584

N_EVEN = (DEPTH + 1) // 2
N_ODD = DEPTH // 2

EVEN_WIDTHS = (RET_HEADS * RET_DK, RET_HEADS * RET_DK, RET_HEADS * RET_DV, RET_HEADS * RET_DV,
               GQA_Q_HEADS * GQA_HEAD_DIM, GQA_KV_HEADS * GQA_HEAD_DIM, GQA_KV_HEADS * GQA_HEAD_DIM)
EVEN_IN = sum(EVEN_WIDTHS)
EVEN_MIX = RET_HEADS * RET_DV + GQA_Q_HEADS * GQA_HEAD_DIM
RWKV_WIDTHS = (RWKV_W, RWKV_W, RWKV_W, 2 * RWKV_W_LORA, 2 * RWKV_A_LORA, RWKV_G_LORA)
RWKV_SHIFT_W = sum(RWKV_WIDTHS)
ODD_WIDTHS = (DIFF_HEADS * 2 * DIFF_HEAD_DIM, DIFF_HEADS * 2 * DIFF_HEAD_DIM, DIFF_HEADS * DIFF_V_DIM, RWKV_SHIFT_W)
ODD_IN = sum(ODD_WIDTHS)
ODD_MIX = DIFF_HEADS * DIFF_V_DIM + RWKV_W

kernel_name = 'hybrid_retention_gqa_diffattn_rwkv7_moe_dit'


def rmsnorm(x, g, eps=NORM_EPS):
    xf = x.astype(jnp.float32)
    y = xf * lax.rsqrt(jnp.mean(xf * xf, axis=-1, keepdims=True) + eps)
    return (y * g.astype(jnp.float32)).astype(x.dtype)


def head_layernorm(y, g, b, eps):
    yf = y.astype(jnp.float32)
    mu = jnp.mean(yf, axis=-1, keepdims=True)
    var = jnp.mean(jnp.square(yf - mu), axis=-1, keepdims=True)
    h, n = y.shape[-2:]
    return (yf - mu) * lax.rsqrt(var + eps) * g.reshape(h, n).astype(jnp.float32) + b.reshape(h, n).astype(jnp.float32)


def modulate(h, shift, scale):
    return h * (1.0 + scale) + shift


def split_cols(p, widths):
    out, start = [], 0
    for w in widths:
        out.append(p[..., start:start + w])
        start += w
    return out


def axial_rope_tables(rows, dim):
    row = jnp.repeat(jnp.arange(rows, dtype=jnp.float32), GRID_W)
    col = jnp.tile(jnp.arange(GRID_W, dtype=jnp.float32), rows)
    n_freq = dim // 4
    freqs = ROPE_THETA ** (-jnp.arange(n_freq, dtype=jnp.float32) / n_freq)
    ang = jnp.concatenate([row[:, None] * freqs, col[:, None] * freqs], axis=-1)
    return jnp.cos(ang), jnp.sin(ang)


def apply_rope(x, cos, sin):
    half = x.shape[-1] // 2
    xf = x.astype(jnp.float32)
    x1, x2 = xf[..., :half], xf[..., half:]
    return jnp.concatenate([x1 * cos - x2 * sin, x1 * sin + x2 * cos], axis=-1).astype(x.dtype)


def sweep_query_blocks(block_fn, q):
    lead = q.shape[:-2]
    n, d = q.shape[-2:]
    nb = n // Q_BLOCK
    qb = jnp.moveaxis(q.reshape(lead + (nb, Q_BLOCK, d)), -3, 0)
    out = jnp.moveaxis(lax.map(block_fn, qb), 0, -3)
    return out.reshape(out.shape[:-3] + (n, out.shape[-1]))


def gqa_attend(q, k, v):
    scale = q.shape[-1] ** -0.5
    def block(qb):
        s = jnp.einsum('bkgqd,bksd->bkgqs', qb, k).astype(jnp.float32) * scale
        p = jax.nn.softmax(s, axis=-1).astype(v.dtype)
        return jnp.einsum('bkgqs,bksd->bkgqd', p, v)
    return sweep_query_blocks(block, q)


def diff_attend(q, k, v, lam):
    scale = q.shape[-1] ** -0.5
    def block(qb):
        s = jnp.einsum('bhmqd,bhmsd->bhmqs', qb, k).astype(jnp.float32) * scale
        p = jax.nn.softmax(s, axis=-1)
        p = p[:, :, 0] - lam * p[:, :, 1]
        return jnp.einsum('bhqs,bhsv->bhqv', p.astype(v.dtype), v)
    return sweep_query_blocks(block, q)


def retention_scan(q, k, v, log_gamma, state0, strict):
    bsz, nh, n, dk = q.shape
    dv = v.shape[-1]
    nc = n // RET_CHUNK
    pos = jnp.arange(RET_CHUNK, dtype=jnp.float32)
    diff = pos[:, None] - pos[None, :]
    mask = diff > 0 if strict else diff >= 0
    inner_decay = jnp.where(mask, jnp.exp(log_gamma[:, None, None] * jnp.maximum(diff, 0.0)), 0.0)
    q_decay = jnp.exp(log_gamma[:, None] * (pos + 1.0))[..., None]
    k_decay = jnp.exp(log_gamma[:, None] * (RET_CHUNK - 1.0 - pos))[..., None]
    chunk_decay = jnp.exp(log_gamma * RET_CHUNK)[:, None, None]

    def to_chunks(t):
        return jnp.moveaxis(t.astype(jnp.float32).reshape(bsz, nh, nc, RET_CHUNK, t.shape[-1]), 2, 0)

    def step(r_state, qkv):
        qc, kc, vc = qkv
        inner = jnp.einsum('bhnd,bhmd->bhnm', qc, kc) * inner_decay
        y = jnp.einsum('bhnm,bhmv->bhnv', inner, vc) + jnp.einsum('bhnd,bhdv->bhnv', qc * q_decay, r_state)
        r_state = r_state * chunk_decay + jnp.einsum('bhmd,bhmv->bhdv', kc * k_decay, vc)
        return r_state, y

    r_final, ys = lax.scan(step, state0, (to_chunks(q), to_chunks(k), to_chunks(v)))
    return jnp.moveaxis(ys, 0, 2).reshape(bsz, nh, n, dv), r_final


def bidirectional_retention(ctx_qkv, lat_qkv, log_gamma):
    bsz, nh, _, dk = ctx_qkv[0].shape
    dv = ctx_qkv[2].shape[-1]
    zero = jnp.zeros((bsz, nh, dk, dv), jnp.float32)
    y_ctx, y_lat = 0.0, 0.0
    for d in range(2):
        f = (lambda t: t[..., ::-1, :]) if d else (lambda t: t)
        yc, state = retention_scan(*[f(t) for t in ctx_qkv], log_gamma[d], zero, strict=bool(d))
        yl, _ = retention_scan(*[f(t) for t in lat_qkv], log_gamma[d], state, strict=bool(d))
        y_ctx = y_ctx + f(yc)
        y_lat = y_lat + f(yl)
    return y_ctx, y_lat


def centred_token_shift(p, mu):
    prev = jnp.pad(p[:, :-1], ((0, 0), (1, 0), (0, 0)))
    nxt = jnp.pad(p[:, 1:], ((0, 0), (0, 1), (0, 0)))
    return p + mu * (0.5 * (prev + nxt) - p)


def rwkv_inputs(p, w0, w2, a0, a2, k_k, k_a):
    bsz, n, _ = p.shape
    r, k, v, wd, ad, gd = split_cols(p, RWKV_WIDTHS)
    def heads(t):
        return t.reshape(bsz, n, RWKV_HEADS, RWKV_HEAD_DIM)
    wd = wd.reshape(bsz, n, 2, RWKV_W_LORA)
    ad = ad.reshape(bsz, n, 2, RWKV_A_LORA)
    kk = heads(k * k_k).astype(jnp.float32)
    kk = kk / jnp.maximum(jnp.sqrt(jnp.sum(kk * kk, axis=-1, keepdims=True)), 1e-12)
    per_dir = []
    for d in range(2):
        wl = (w0[d] + jnp.tanh(wd[:, :, d]) @ w2[d]).astype(jnp.float32)
        decay = jnp.exp(-jnp.exp(-jax.nn.softplus(-wl) - 0.5))
        a = jax.nn.sigmoid(a0[d] + ad[:, :, d] @ a2[d])
        kd = k * (1.0 + (a - 1.0) * k_a)
        per_dir.append((heads(r), heads(decay), heads(kd), heads(v), -kk, kk * heads(a).astype(jnp.float32)))
    return per_dir, (heads(r), heads(k), heads(v), gd)


def rwkv_scan(r, w, k, v, a, b, state0, read_before):
    def step(s, inp):
        rt, wt, kt, vt, at, bt = inp
        s_new = (s * wt[:, :, None, :]
                 + jnp.einsum('bhvk,bhk->bhv', s, at)[..., None] * bt[:, :, None, :]
                 + vt[..., None] * kt[:, :, None, :])
        y = jnp.einsum('bhvk,bhk->bhv', s if read_before else s_new, rt)
        return s_new, y
    xs = tuple(jnp.moveaxis(t.astype(jnp.float32), 1, 0) for t in (r, w, k, v, a, b))
    s_final, ys = lax.scan(step, state0, xs)
    return jnp.moveaxis(ys, 0, 1), s_final


def rwkv_branch(pc, pl, w0, w2, a0, a2, g2, k_k, k_a, r_k, ln_g, ln_b, need_ctx):
    ctx_dirs, ctx_extra = rwkv_inputs(pc, w0, w2, a0, a2, k_k, k_a)
    lat_dirs, lat_extra = rwkv_inputs(pl, w0, w2, a0, a2, k_k, k_a)
    bsz = pl.shape[0]
    zero = jnp.zeros((bsz, RWKV_HEADS, RWKV_HEAD_DIM, RWKV_HEAD_DIM), jnp.float32)
    y_ctx, y_lat = 0.0, 0.0
    for d in range(2):
        f = (lambda t: t[:, ::-1]) if d else (lambda t: t)
        yc, state = rwkv_scan(*[f(t) for t in ctx_dirs[d]], zero, read_before=bool(d))
        yl, _ = rwkv_scan(*[f(t) for t in lat_dirs[d]], state, read_before=bool(d))
        y_ctx = y_ctx + f(yc)
        y_lat = y_lat + f(yl)

    def finish(y, extra):
        r, k, v, gd = extra
        n = r.shape[1]
        y = head_layernorm(y, ln_g, ln_b, RWKV_LN_EPS).astype(r.dtype)
        bonus = jnp.sum(r * k * r_k, axis=-1, keepdims=True) * v
        return (y + bonus).reshape(bsz, n, RWKV_W) * (jax.nn.sigmoid(gd) @ g2)

    return (finish(y_ctx, ctx_extra) if need_ctx else None), finish(y_lat, lat_extra)


def even_mixer(hc, hl, rope_ret, rope_gqa, w_in, w_out, decay_exp, ret_g, q_g, k_g, need_ctx):
    bsz = hl.shape[0]
    log_gamma = jnp.log1p(-jnp.exp2(-decay_exp.astype(jnp.float32)))

    def project(h, rope_r, rope_a):
        n = h.shape[1]
        rq, rk, rv, rg, aq, ak, av = split_cols(h @ w_in, EVEN_WIDTHS)
        def heads(t, nh, d):
            return t.reshape(bsz, n, nh, d).transpose(0, 2, 1, 3)
        rq = heads(rq, RET_HEADS, RET_DK)
        rk = heads(rk, RET_HEADS, RET_DK) * RET_DK ** -0.5
        rv = heads(rv, RET_HEADS, RET_DV)
        aq = heads(rmsnorm(aq.reshape(bsz, n, GQA_Q_HEADS, GQA_HEAD_DIM), q_g), GQA_Q_HEADS, GQA_HEAD_DIM)
        ak = heads(rmsnorm(ak.reshape(bsz, n, GQA_KV_HEADS, GQA_HEAD_DIM), k_g), GQA_KV_HEADS, GQA_HEAD_DIM)
        av = heads(av, GQA_KV_HEADS, GQA_HEAD_DIM)
        if rope_r is not None:
            rq, rk = apply_rope(rq, *rope_r), apply_rope(rk, *rope_r)
            aq, ak = apply_rope(aq, *rope_a), apply_rope(ak, *rope_a)
        aq = aq.reshape(bsz, GQA_KV_HEADS, GQA_GROUP, n, GQA_HEAD_DIM)
        return (rq, rk, rv), rg, (aq, ak, av)

    ctx_ret, ctx_gate, (cq, ck, cv) = project(hc, None, None)
    lat_ret, lat_gate, (lq, lk, lv) = project(hl, rope_ret, rope_gqa)
    ret_ctx, ret_lat = bidirectional_retention(ctx_ret, lat_ret, log_gamma)

    def ret_out(y, gate):
        n = y.shape[2]
        y = rmsnorm(y.transpose(0, 2, 1, 3), ret_g.reshape(RET_HEADS, RET_DV))
        return y.reshape(bsz, n, RET_HEADS * RET_DV).astype(gate.dtype) * jax.nn.silu(gate)

    def gqa_out(o):
        n = o.shape[3]
        o = o.reshape(bsz, GQA_Q_HEADS, n, GQA_HEAD_DIM).transpose(0, 2, 1, 3)
        return o.reshape(bsz, n, GQA_Q_HEADS * GQA_HEAD_DIM)

    attn_lat = gqa_attend(lq, jnp.concatenate([ck, lk], axis=2), jnp.concatenate([cv, lv], axis=2))
    y_lat = jnp.concatenate([ret_out(ret_lat, lat_gate), gqa_out(attn_lat)], axis=-1) @ w_out
    y_ctx = None
    if need_ctx:
        y_ctx = jnp.concatenate([ret_out(ret_ctx, ctx_gate), gqa_out(gqa_attend(cq, ck, cv))], axis=-1) @ w_out
    return y_ctx, y_lat


def odd_mixer(hc, hl, rope_diff, w_in, w_out, lam_vec, subln_g, lam_init, mu, w0, w2, a0, a2, g2,
              k_k, k_a, r_k, ln_g, ln_b, need_ctx):
    bsz = hl.shape[0]
    lamv = lam_vec.astype(jnp.float32)
    lam = jnp.exp(jnp.sum(lamv[0] * lamv[1])) - jnp.exp(jnp.sum(lamv[2] * lamv[3])) + lam_init

    def project(h, rope):
        n = h.shape[1]
        dq, dk, dv, rw = split_cols(h @ w_in, ODD_WIDTHS)
        q = dq.reshape(bsz, n, DIFF_HEADS, 2, DIFF_HEAD_DIM).transpose(0, 2, 3, 1, 4)
        k = dk.reshape(bsz, n, DIFF_HEADS, 2, DIFF_HEAD_DIM).transpose(0, 2, 3, 1, 4)
        v = dv.reshape(bsz, n, DIFF_HEADS, DIFF_V_DIM).transpose(0, 2, 1, 3)
        if rope is not None:
            q, k = apply_rope(q, *rope), apply_rope(k, *rope)
        return (q, k, v), centred_token_shift(rw, mu)

    (cq, ck, cv), c_rw = project(hc, None)
    (lq, lk, lv), l_rw = project(hl, rope_diff)
    rwkv_ctx, rwkv_lat = rwkv_branch(c_rw, l_rw, w0, w2, a0, a2, g2, k_k, k_a, r_k, ln_g, ln_b, need_ctx)

    def diff_out(o):
        n = o.shape[2]
        o = rmsnorm(o.transpose(0, 2, 1, 3), subln_g) * (1.0 - lam_init)
        return o.reshape(bsz, n, DIFF_HEADS * DIFF_V_DIM)

    attn_lat = diff_attend(lq, jnp.concatenate([ck, lk], axis=3), jnp.concatenate([cv, lv], axis=2), lam)
    y_lat = jnp.concatenate([diff_out(attn_lat), rwkv_lat], axis=-1) @ w_out
    y_ctx = None
    if need_ctx:
        y_ctx = jnp.concatenate([diff_out(diff_attend(cq, ck, cv, lam)), rwkv_ctx], axis=-1) @ w_out
    return y_ctx, y_lat


def swiglu(h, wg, wu, wd):
    return (jax.nn.silu(h @ wg) * (h @ wu)) @ wd


def moe_swiglu(h, router, wg, wu, wd):
    logits = (h @ router).astype(jnp.float32)
    top_logits, top_idx = lax.top_k(logits, TOP_K)
    top_w = jax.nn.softmax(top_logits, axis=-1)
    gates = jnp.einsum('bske,bsk->bse', jax.nn.one_hot(top_idx, N_EXPERTS, dtype=jnp.float32), top_w).astype(h.dtype)
    out = jnp.zeros_like(h)
    for e in range(N_EXPERTS):
        out = out + gates[..., e:e + 1] * swiglu(h, wg[e], wu[e], wd[e])
    return out


def setup_inputs(seed: int = 0) -> dict:
    key = jax.random.key(seed)
    ks = iter(jax.random.split(key, 40))
    f32 = jnp.float32
    def nrm(shape, scale):
        return scale * jax.random.normal(next(ks), shape, f32)
    D = D_MODEL
    ramp = jnp.arange(RWKV_W, dtype=f32) / (RWKV_W - 1)
    return {
        'x': nrm((BATCH, SEQ, D), 1.0),
        'c': nrm((BATCH, D), 1.0),
        'ctx': nrm((BATCH, CTX_LEN, D), 1.0),
        'c_ctx': nrm((D,), 1.0),
        'mod_w': nrm((DEPTH, D, 6 * D), 0.5 * D ** -0.5),
        'mod_b': nrm((DEPTH, 6 * D), 0.02),
        'norm_g': 1.0 + nrm((DEPTH, 4, D), 0.05),
        'even_w_in': nrm((N_EVEN, D, EVEN_IN), D ** -0.5),
        'even_w_out': nrm((N_EVEN, EVEN_MIX, D), EVEN_MIX ** -0.5),
        'ret_decay_exp': 5.0 + jnp.arange(RET_HEADS, dtype=f32) + nrm((N_EVEN, 2, RET_HEADS), 0.1),
        'ret_norm_g': 1.0 + nrm((N_EVEN, RET_HEADS * RET_DV), 0.05),
        'gqa_q_norm': 1.0 + nrm((N_EVEN, GQA_HEAD_DIM), 0.05),
        'gqa_k_norm': 1.0 + nrm((N_EVEN, GQA_HEAD_DIM), 0.05),
        'ffn_w_gate': nrm((N_EVEN, D, FFN_HIDDEN), D ** -0.5),
        'ffn_w_up': nrm((N_EVEN, D, FFN_HIDDEN), D ** -0.5),
        'ffn_w_down': nrm((N_EVEN, FFN_HIDDEN, D), FFN_HIDDEN ** -0.5),
        'odd_w_in': nrm((N_ODD, D, ODD_IN), D ** -0.5),
        'odd_w_out': nrm((N_ODD, ODD_MIX, D), ODD_MIX ** -0.5),
        'diff_lambda': nrm((N_ODD, 4, DIFF_HEAD_DIM), 0.1),
        'diff_subln_g': 1.0 + nrm((N_ODD, DIFF_V_DIM), 0.05),
        'rwkv_mu': jax.random.uniform(next(ks), (N_ODD, RWKV_SHIFT_W), f32),
        'rwkv_w0': -6.0 + 5.0 * ramp ** 0.85 + nrm((N_ODD, 2, RWKV_W), 0.1),
        'rwkv_w2': nrm((N_ODD, 2, RWKV_W_LORA, RWKV_W), 0.5 * RWKV_W_LORA ** -0.5),
        'rwkv_a0': nrm((N_ODD, 2, RWKV_W), 0.1),
        'rwkv_a2': nrm((N_ODD, 2, RWKV_A_LORA, RWKV_W), 0.5 * RWKV_A_LORA ** -0.5),
        'rwkv_g2': nrm((N_ODD, RWKV_G_LORA, RWKV_W), RWKV_G_LORA ** -0.5),
        'rwkv_k_k': 0.85 + nrm((N_ODD, RWKV_W), 0.05),
        'rwkv_k_a': 1.0 + nrm((N_ODD, RWKV_W), 0.05),
        'rwkv_r_k': nrm((N_ODD, RWKV_HEADS, RWKV_HEAD_DIM), 0.1),
        'rwkv_ln_g': 1.0 + nrm((N_ODD, RWKV_W), 0.05),
        'rwkv_ln_b': nrm((N_ODD, RWKV_W), 0.02),
        'moe_router': nrm((N_ODD, D, N_EXPERTS), D ** -0.5),
        'moe_w_gate': nrm((N_ODD, N_EXPERTS, D, EXPERT_HIDDEN), D ** -0.5),
        'moe_w_up': nrm((N_ODD, N_EXPERTS, D, EXPERT_HIDDEN), D ** -0.5),
        'moe_w_down': nrm((N_ODD, N_EXPERTS, EXPERT_HIDDEN, D), EXPERT_HIDDEN ** -0.5),
    }


def reference(x, c, ctx, c_ctx, mod_w, mod_b, norm_g, even_w_in, even_w_out, ret_decay_exp, ret_norm_g,
              gqa_q_norm, gqa_k_norm, ffn_w_gate, ffn_w_up, ffn_w_down, odd_w_in, odd_w_out, diff_lambda,
              diff_subln_g, rwkv_mu, rwkv_w0, rwkv_w2, rwkv_a0, rwkv_a2, rwkv_g2, rwkv_k_k, rwkv_k_a, rwkv_r_k,
              rwkv_ln_g, rwkv_ln_b, moe_router, moe_w_gate, moe_w_up, moe_w_down):
    rows = x.shape[1] // GRID_W
    rope_ret = axial_rope_tables(rows, RET_DK)
    rope_gqa = axial_rope_tables(rows, GQA_HEAD_DIM)
    rope_diff = axial_rope_tables(rows, DIFF_HEAD_DIM)
    cond_lat = jax.nn.silu(c)
    cond_ctx = jax.nn.silu(c_ctx)

    for layer in range(DEPTH):
        i = layer // 2
        need_ctx = layer < DEPTH - 1
        l_sh1, l_sc1, l_g1, l_sh2, l_sc2, l_g2 = jnp.split((cond_lat @ mod_w[layer] + mod_b[layer])[:, None, :], 6, axis=-1)
        c_sh1, c_sc1, c_g1, c_sh2, c_sc2, c_g2 = jnp.split(cond_ctx @ mod_w[layer] + mod_b[layer], 6, axis=-1)

        hl = modulate(rmsnorm(x, norm_g[layer, 0]), l_sh1, l_sc1)
        hc = modulate(rmsnorm(ctx, norm_g[layer, 0]), c_sh1, c_sc1)
        if layer % 2 == 0:
            y_ctx, y_lat = even_mixer(hc, hl, rope_ret, rope_gqa, even_w_in[i], even_w_out[i], ret_decay_exp[i],
                                      ret_norm_g[i], gqa_q_norm[i], gqa_k_norm[i], need_ctx)
            ffn = lambda h: swiglu(h, ffn_w_gate[i], ffn_w_up[i], ffn_w_down[i])
        else:
            lam_init = 0.8 - 0.6 * math.exp(-0.3 * layer)
            y_ctx, y_lat = odd_mixer(hc, hl, rope_diff, odd_w_in[i], odd_w_out[i], diff_lambda[i], diff_subln_g[i],
                                     lam_init, rwkv_mu[i], rwkv_w0[i], rwkv_w2[i], rwkv_a0[i], rwkv_a2[i], rwkv_g2[i],
                                     rwkv_k_k[i], rwkv_k_a[i], rwkv_r_k[i], rwkv_ln_g[i], rwkv_ln_b[i], need_ctx)
            ffn = lambda h: moe_swiglu(h, moe_router[i], moe_w_gate[i], moe_w_up[i], moe_w_down[i])

        x = x + l_g1 * rmsnorm(y_lat, norm_g[layer, 1])
        x = x + l_g2 * rmsnorm(ffn(modulate(rmsnorm(x, norm_g[layer, 2]), l_sh2, l_sc2)), norm_g[layer, 3])
        if need_ctx:
            ctx = ctx + c_g1 * rmsnorm(y_ctx, norm_g[layer, 1])
            ctx = ctx + c_g2 * rmsnorm(ffn(modulate(rmsnorm(ctx, norm_g[layer, 2]), c_sh2, c_sc2)), norm_g[layer, 3])
    return x
```

```python
import functools
import math

import jax
import jax.numpy as jnp
from jax import lax
from jax.experimental import pallas as pl
from jax.experimental.pallas import tpu as pltpu

F32 = jnp.float32
BF16 = jnp.bfloat16

GRID_W = 64
ROPE_THETA = 10000.0
NORM_EPS = 1e-6
RWKV_LN_EPS = 64e-5
RET_HEADS = 4
RET_DK = 128
RET_DV = 128
GQA_Q_HEADS = 4
GQA_KV_HEADS = 2
GQA_HEAD_DIM = 128
DIFF_HEADS = 4
DIFF_HEAD_DIM = 64
DIFF_V_DIM = 2 * DIFF_HEAD_DIM
RWKV_HEADS = 8
RWKV_HEAD_DIM = 64
RWKV_W = RWKV_HEADS * RWKV_HEAD_DIM
RWKV_W_LORA = 64
RWKV_A_LORA = 64
RWKV_G_LORA = 128
N_EXPERTS = 8
TOP_K = 2

V7X_VMEM_LIMIT_BYTES = 56 * 1024 * 1024
LANES = 128

RET_CHUNK = 128
RWKV_CHUNK = 64


def _params(*sem):
    return pltpu.CompilerParams(dimension_semantics=sem, vmem_limit_bytes=V7X_VMEM_LIMIT_BYTES)


def _nt(a, b):
    return lax.dot_general(a, b, (((1,), (1,)), ((), ())), preferred_element_type=F32)


def _tn(a, b):
    return lax.dot_general(a, b, (((0,), (0,)), ((), ())), preferred_element_type=F32)


def _mm(a, b):
    return jnp.dot(a, b, preferred_element_type=F32)


def _sigmoid(x):
    return 1.0 / (1.0 + jnp.exp(-x))


def _silu(x):
    return x * _sigmoid(x)


def _rms(x, eps=NORM_EPS):
    return x * lax.rsqrt(jnp.mean(x * x, axis=-1, keepdims=True) + eps)


def _mod_kernel(c_ref, w_ref, b_ref, o_ref):
    c = _silu(c_ref[...])
    o_ref[...] = _mm(c.astype(BF16), w_ref[...].astype(BF16)) + b_ref[...]


def _modulation(cond, mod_w, mod_b, tn=1536):
    depth, d, n = mod_w.shape
    rows = cond.shape[0]
    return pl.pallas_call(
        _mod_kernel,
        out_shape=jax.ShapeDtypeStruct((depth, rows, n), F32),
        grid=(depth, n // tn),
        in_specs=[
            pl.BlockSpec((rows, d), lambda l, j: (0, 0)),
            pl.BlockSpec((None, d, tn), lambda l, j: (l, 0, j)),
            pl.BlockSpec((None, 1, tn), lambda l, j: (l, 0, j)),
        ],
        out_specs=pl.BlockSpec((None, rows, tn), lambda l, j: (l, 0, j)),
        compiler_params=_params("parallel", "parallel"),
        name="modulation",
    )(cond, mod_w, mod_b.reshape(depth, 1, n))


def _proj_in_kernel(x_ref, g_ref, sh_ref, sc_ref, w_ref, o_ref):
    h = _rms(x_ref[...]) * g_ref[...]
    h = h * (1.0 + sc_ref[...]) + sh_ref[...]
    o_ref[...] = _mm(h.astype(BF16), w_ref[...])


def _proj_in(x, g, shift, scale, w, tm):
    b, s, d = x.shape
    n = w.shape[1]
    tm = min(tm, s)
    return pl.pallas_call(
        _proj_in_kernel,
        out_shape=jax.ShapeDtypeStruct((b, s, n), F32),
        grid=(b, s // tm),
        in_specs=[
            pl.BlockSpec((None, tm, d), lambda i, j: (i, j, 0)),
            pl.BlockSpec((1, d), lambda i, j: (0, 0)),
            pl.BlockSpec((None, 1, d), lambda i, j: (i, 0, 0)),
            pl.BlockSpec((None, 1, d), lambda i, j: (i, 0, 0)),
            pl.BlockSpec((d, n), lambda i, j: (0, 0)),
        ],
        out_specs=pl.BlockSpec((None, tm, n), lambda i, j: (i, j, 0)),
        compiler_params=_params("parallel", "parallel"),
        name="proj_in",
    )(x, g.reshape(1, d), shift.reshape(b, 1, d), scale.reshape(b, 1, d), w)


def _proj_out_kernel(a_ref, w_ref, x_ref, g_ref, gate_ref, o_ref):
    y = _mm(a_ref[...], w_ref[...])
    o_ref[...] = x_ref[...] + gate_ref[...] * (_rms(y) * g_ref[...])


def _proj_out(a, w, x, g, gate, tm):
    b, s, k = a.shape
    d = w.shape[1]
    tm = min(tm, s)
    return pl.pallas_call(
        _proj_out_kernel,
        out_shape=jax.ShapeDtypeStruct((b, s, d), F32),
        grid=(b, s // tm),
        in_specs=[
            pl.BlockSpec((None, tm, k), lambda i, j: (i, j, 0)),
            pl.BlockSpec((k, d), lambda i, j: (0, 0)),
            pl.BlockSpec((None, tm, d), lambda i, j: (i, j, 0)),
            pl.BlockSpec((1, d), lambda i, j: (0, 0)),
            pl.BlockSpec((None, 1, d), lambda i, j: (i, 0, 0)),
        ],
        out_specs=pl.BlockSpec((None, tm, d), lambda i, j: (i, j, 0)),
        compiler_params=_params("parallel", "parallel"),
        name="proj_out",
    )(a, w, x, g.reshape(1, d), gate.reshape(b, 1, d))


def _attn_kernel(q_ref, k_ref, v_ref, o_ref):
    s = _nt(q_ref[...], k_ref[...])
    p = jnp.exp(s - jnp.max(s, axis=-1, keepdims=True))
    l = jnp.sum(p, axis=-1, keepdims=True)
    o_ref[...] = _mm(p.astype(BF16), v_ref[...]) / l


def _attention(q, k, v, q_per_k, q_per_v, tq):
    b, hq, sq, d = q.shape
    sk = k.shape[2]
    dv = v.shape[3]
    tq = min(tq, sq)
    return pl.pallas_call(
        _attn_kernel,
        out_shape=jax.ShapeDtypeStruct((b, hq, sq, dv), F32),
        grid=(b, hq, sq // tq),
        in_specs=[
            pl.BlockSpec((None, None, tq, d), lambda i, h, j: (i, h, j, 0)),
            pl.BlockSpec((None, None, sk, d), lambda i, h, j: (i, h // q_per_k, 0, 0)),
            pl.BlockSpec((None, None, sk, dv), lambda i, h, j: (i, h // q_per_v, 0, 0)),
        ],
        out_specs=pl.BlockSpec((None, None, tq, dv), lambda i, h, j: (i, h, j, 0)),
        compiler_params=_params("parallel", "parallel", "parallel"),
        name="attention",
    )(q, k, v)


def _scan_chunk(direction, step, n_ctx, n_total):
    backward = jnp.where(step < n_ctx, n_ctx - 1 - step, n_total - 1 - step + n_ctx)
    return jnp.where(direction == 0, step, backward)


def _retention_kernel(lg_ref, q_ref, k_ref, v_ref, y_ref, st_ref):
    h = pl.program_id(1)
    direction = pl.program_id(2)
    step = pl.program_id(3)

    @pl.when(step == 0)
    def _():
        st_ref[...] = jnp.zeros_like(st_ref)

    c = q_ref.shape[0]
    fwd = direction == 0
    lg = lg_ref[direction, h]
    row = lax.broadcasted_iota(jnp.int32, (c, c), 0)
    col = lax.broadcasted_iota(jnp.int32, (c, c), 1)
    ahead = jnp.where(fwd, row - col, col - row)
    mask = ahead >= jnp.where(fwd, 0, 1)
    inner_decay = jnp.where(mask, jnp.exp(lg * jnp.maximum(ahead, 0).astype(F32)), 0.0)
    pos = lax.broadcasted_iota(jnp.int32, (c, 1), 0)
    pos = jnp.where(fwd, pos, c - 1 - pos).astype(F32)
    q_decay = jnp.exp(lg * (pos + 1.0))
    k_decay = jnp.exp(lg * (c - 1.0 - pos))
    chunk_decay = jnp.exp(lg * jnp.full((1, 1), c, F32))

    q = q_ref[...]
    k = k_ref[...]
    v = v_ref[...].astype(BF16)
    st = st_ref[...]
    inner = _nt(q.astype(BF16), k.astype(BF16)) * inner_decay
    y_ref[...] = _mm(inner.astype(BF16), v) + _mm((q * q_decay).astype(BF16), st.astype(BF16))
    st_ref[...] = st * chunk_decay + _tn((k * k_decay).astype(BF16), v)


def _retention(q, k, v, log_gamma, n_ctx_tokens):
    b, h, s, dk = q.shape
    dv = v.shape[3]
    c = RET_CHUNK
    n_total = s // c
    n_ctx = n_ctx_tokens // c

    def seq_map(i, j, d, t):
        return (i, j, _scan_chunk(d, t, n_ctx, n_total), 0)

    return pl.pallas_call(
        _retention_kernel,
        out_shape=jax.ShapeDtypeStruct((2, b, h, s, dv), F32),
        grid=(b, h, 2, n_total),
        in_specs=[
            pl.BlockSpec(memory_space=pltpu.SMEM),
            pl.BlockSpec((None, None, c, dk), seq_map),
            pl.BlockSpec((None, None, c, dk), seq_map),
            pl.BlockSpec((None, None, c, dv), seq_map),
        ],
        out_specs=pl.BlockSpec(
            (None, None, None, c, dv),
            lambda i, j, d, t: (d, i, j, _scan_chunk(d, t, n_ctx, n_total), 0)),
        scratch_shapes=[pltpu.VMEM((dk, dv), F32)],
        compiler_params=_params("parallel", "parallel", "parallel", "arbitrary"),
        name="retention",
    )(log_gamma, q, k, v)


def _rwkv_kernel(p_ref, w0_ref, w2_ref, a0_ref, a2_ref, kk_ref, ka_ref, y_ref, st_ref):
    direction = pl.program_id(1)
    step = pl.program_id(2)

    @pl.when(step == 0)
    def _():
        st_ref[...] = jnp.zeros_like(st_ref)

    c = RWKV_CHUNK
    w = RWKV_W
    fwd = direction == 0
    x = p_ref[...]
    r = x[:, 0:w]
    k = x[:, w:2 * w]
    v = x[:, 2 * w:3 * w]
    wd = x[:, 3 * w:3 * w + 2 * RWKV_W_LORA]
    ad = x[:, 3 * w + 2 * RWKV_W_LORA:3 * w + 2 * RWKV_W_LORA + 2 * RWKV_A_LORA]

    wl = w0_ref[...] + _mm(jnp.tanh(wd).astype(BF16), w2_ref[...])
    logw = -math.exp(-0.5) * _sigmoid(wl)
    a = _sigmoid(a0_ref[...] + _mm(ad.astype(BF16), a2_ref[...]))

    li = lax.broadcasted_iota(jnp.int32, (LANES, LANES), 0)
    lj = lax.broadcasted_iota(jnp.int32, (LANES, LANES), 1)
    same_head = ((li // RWKV_HEAD_DIM) == (lj // RWKV_HEAD_DIM))
    seg = jnp.where(same_head, 1.0, 0.0).astype(BF16)

    def head_sum(t):
        hi = t.astype(BF16)
        lo = (t - hi.astype(F32)).astype(BF16)
        parts = []
        for j in range(w // LANES):
            sl = slice(j * LANES, (j + 1) * LANES)
            parts.append(_mm(hi[:, sl], seg) + _mm(lo[:, sl], seg))
        return jnp.concatenate(parts, axis=1)

    kkr = k * kk_ref[...]
    kk = kkr / jnp.maximum(jnp.sqrt(head_sum(kkr * kkr)), 1e-12)
    kd = k * (1.0 + (a - 1.0) * ka_ref[...])
    av = -kk
    bv = kk * a

    ti = lax.broadcasted_iota(jnp.int32, (c, c), 0)
    tj = lax.broadcasted_iota(jnp.int32, (c, c), 1)
    tri = jnp.where(jnp.where(fwd, ti - tj, tj - ti) >= 0, 1.0, 0.0).astype(BF16)
    l1 = logw.astype(BF16)
    rem = logw - l1.astype(F32)
    l2 = rem.astype(BF16)
    l3 = (rem - l2.astype(F32)).astype(BF16)
    lp = _mm(tri, l1) + _mm(tri, l2) + _mm(tri, l3)
    lpp = lp - logw
    lpc = jnp.sum(logw, axis=0, keepdims=True)

    e_prev = jnp.exp(lpp)
    e_inv = jnp.exp(-lp)
    e_tail = jnp.exp(lpc - lp)
    at = av * e_prev
    bt = bv * e_inv
    kt = kd * e_inv
    rt = r * jnp.where(fwd, jnp.exp(lp), e_prev)
    bh = bv * e_tail
    kh = kd * e_tail
    pc = jnp.exp(lpc)

    mi = lax.broadcasted_iota(jnp.int32, (2 * c, 2 * c), 0)
    mj = lax.broadcasted_iota(jnp.int32, (2 * c, 2 * c), 1)
    ahead = jnp.where(fwd, mi - mj, mj - mi)
    ahead = jnp.where((mi // c) == (mj // c), ahead, -1)
    m_strict = ahead > 0
    m_out = ahead >= jnp.where(fwd, 0, 1)
    eye = mi == mj
    lane = lax.broadcasted_iota(jnp.int32, (c, LANES), 1)
    lo_half = lane < RWKV_HEAD_DIM

    for j in range(w // LANES):
        sl = slice(j * LANES, (j + 1) * LANES)

        def expand(t):
            tp = t[:, sl]
            return jnp.concatenate(
                [jnp.where(lo_half, tp, 0.0), jnp.where(lo_half, 0.0, tp)], axis=0).astype(BF16)

        def collapse(t):
            return t[:c] + t[c:]

        ae, be, ke, re = expand(at), expand(bt), expand(kt), expand(rt)
        bhe, khe, ve = expand(bh), expand(kh), expand(v)
        big = _nt(jnp.concatenate([ae, re], axis=0), jnp.concatenate([be, ke], axis=0))
        mab = jnp.where(m_strict, big[:2 * c, :2 * c], 0.0)
        mak = jnp.where(m_strict, big[:2 * c, 2 * c:], 0.0)
        nrb = jnp.where(m_out, big[2 * c:, :2 * c], 0.0).astype(BF16)
        nrk = jnp.where(m_out, big[2 * c:, 2 * c:], 0.0).astype(BF16)

        t_inv = jnp.where(eye, 1.0, 0.0) + mab
        mp = mab.astype(BF16)
        for _ in range(int(math.log2(c)) - 1):
            mp = _mm(mp, mp).astype(BF16)
            t_inv = t_inv + _mm(mp, t_inv.astype(BF16))
        tb = t_inv.astype(BF16)

        w2 = _mm(tb, _mm(mak.astype(BF16), ve).astype(BF16)).astype(BF16)
        ta = _mm(tb, ae).astype(BF16)
        wv = jnp.concatenate([w2, ve], axis=0)
        phi = jnp.where(eye, jnp.broadcast_to(pc[:, sl], (2 * c, LANES)), 0.0) + _tn(bhe, ta)
        psi = _tn(jnp.concatenate([bhe, khe], axis=0), wv)
        qp = rt[:, sl] + collapse(_mm(nrb, ta))
        zp = collapse(_mm(jnp.concatenate([nrb, nrk], axis=1), wv))

        sb = st_ref[j].astype(BF16)
        y_ref[:, sl] = _mm(qp.astype(BF16), sb) + zp
        st_ref[j] = _mm(phi.astype(BF16), sb) + psi


def _rwkv_scan(p_all, w0, w2p, a0, a2p, k_k, k_a, n_ctx_tokens):
    b, s, pw = p_all.shape
    c = RWKV_CHUNK
    w = RWKV_W
    n_total = s // c
    n_ctx = n_ctx_tokens // c

    def seq_map(i, d, t):
        return (i, _scan_chunk(d, t, n_ctx, n_total), 0)

    dir_vec = pl.BlockSpec((None, 1, w), lambda i, d, t: (d, 0, 0))
    dir_mat = pl.BlockSpec((None, LANES, w), lambda i, d, t: (d, 0, 0))
    vec = pl.BlockSpec((1, w), lambda i, d, t: (0, 0))
    return pl.pallas_call(
        _rwkv_kernel,
        out_shape=jax.ShapeDtypeStruct((b, 2, s, w), F32),
        grid=(b, 2, n_total),
        in_specs=[pl.BlockSpec((None, c, pw), seq_map), dir_vec, dir_mat, dir_vec, dir_mat, vec, vec],
        out_specs=pl.BlockSpec(
            (None, None, c, w), lambda i, d, t: (i, d, _scan_chunk(d, t, n_ctx, n_total), 0)),
        scratch_shapes=[pltpu.VMEM((w // LANES, LANES, LANES), F32)],
        compiler_params=_params("parallel", "parallel", "arbitrary"),
        name="rwkv7",
    )(p_all, w0.reshape(2, 1, w), w2p, a0.reshape(2, 1, w), a2p, k_k.reshape(1, w), k_a.reshape(1, w))


def _ffn_kernel(x_ref, gi_ref, sh_ref, sc_ref, wg_ref, wu_ref, wd_ref, go_ref, gate_ref, o_ref,
                h_ref, acc_ref):
    j = pl.program_id(2)

    @pl.when(j == 0)
    def _():
        h = _rms(x_ref[...]) * gi_ref[...]
        h_ref[...] = (h * (1.0 + sc_ref[...]) + sh_ref[...]).astype(BF16)
        acc_ref[...] = jnp.zeros_like(acc_ref)

    h = h_ref[...]
    act = _silu(_mm(h, wg_ref[...])) * _mm(h, wu_ref[...])
    acc_ref[...] += _mm(act.astype(BF16), wd_ref[...])

    @pl.when(j == pl.num_programs(2) - 1)
    def _():
        o_ref[...] = x_ref[...] + gate_ref[...] * (_rms(acc_ref[...]) * go_ref[...])


def _ffn(x, g_in, shift, scale, wg, wu, wd, g_out, gate, tm, th):
    b, s, d = x.shape
    hid = wg.shape[1]
    tm = min(tm, s)
    row = pl.BlockSpec((None, tm, d), lambda i, m, j: (i, m, 0))
    vec = pl.BlockSpec((1, d), lambda i, m, j: (0, 0))
    bvec = pl.BlockSpec((None, 1, d), lambda i, m, j: (i, 0, 0))
    return pl.pallas_call(
        _ffn_kernel,
        out_shape=jax.ShapeDtypeStruct((b, s, d), F32),
        grid=(b, s // tm, hid // th),
        in_specs=[
            row, vec, bvec, bvec,
            pl.BlockSpec((d, th), lambda i, m, j: (0, j)),
            pl.BlockSpec((d, th), lambda i, m, j: (0, j)),
            pl.BlockSpec((th, d), lambda i, m, j: (j, 0)),
            vec, bvec,
        ],
        out_specs=row,
        scratch_shapes=[pltpu.VMEM((tm, d), BF16), pltpu.VMEM((tm, d), F32)],
        compiler_params=_params("parallel", "parallel", "arbitrary"),
        name="ffn",
    )(x, g_in.reshape(1, d), shift.reshape(b, 1, d), scale.reshape(b, 1, d), wg, wu, wd,
      g_out.reshape(1, d), gate.reshape(b, 1, d))


def _moe_kernel(x_ref, gi_ref, sh_ref, sc_ref, rt_ref, wg_ref, wu_ref, wd_ref, go_ref, gate_ref,
                o_ref, h_ref, gates_ref, acc_ref):
    e = pl.program_id(2)
    j = pl.program_id(3)

    @pl.when((e == 0) & (j == 0))
    def _():
        h = _rms(x_ref[...]) * gi_ref[...]
        h = h * (1.0 + sc_ref[...]) + sh_ref[...]
        hb = h.astype(BF16)
        h_ref[...] = hb
        acc_ref[...] = jnp.zeros_like(acc_ref)
        rt = rt_ref[...]
        rt_hi = rt.astype(BF16)
        rt_lo = (rt - rt_hi.astype(F32)).astype(BF16)
        h_lo = (h - hb.astype(F32)).astype(BF16)
        logits = _mm(hb, rt_hi) + _mm(hb, rt_lo) + _mm(h_lo, rt_hi)
        lane = lax.broadcasted_iota(jnp.int32, logits.shape, 1)
        neg = jnp.float32(-jnp.inf)
        logits = jnp.where(lane < N_EXPERTS, logits, neg)
        m1 = jnp.max(logits, axis=-1, keepdims=True)
        i1 = jnp.min(jnp.where(logits == m1, lane, LANES), axis=-1, keepdims=True)
        rest = jnp.where(lane == i1, neg, logits)
        m2 = jnp.max(rest, axis=-1, keepdims=True)
        i2 = jnp.min(jnp.where(rest == m2, lane, LANES), axis=-1, keepdims=True)
        p2 = jnp.exp(m2 - m1)
        w1 = 1.0 / (1.0 + p2)
        gates_ref[...] = jnp.where(lane == i1, w1, 0.0) + jnp.where(lane == i2, p2 * w1, 0.0)

    h = h_ref[...]
    lane = lax.broadcasted_iota(jnp.int32, gates_ref.shape, 1)
    ge = jnp.sum(jnp.where(lane == e, gates_ref[...], 0.0), axis=-1, keepdims=True)
    act = _silu(_mm(h, wg_ref[...])) * _mm(h, wu_ref[...]) * ge
    acc_ref[...] += _mm(act.astype(BF16), wd_ref[...])

    @pl.when((e == pl.num_programs(2) - 1) & (j == pl.num_programs(3) - 1))
    def _():
        o_ref[...] = x_ref[...] + gate_ref[...] * (_rms(acc_ref[...]) * go_ref[...])


def _moe(x, g_in, shift, scale, router, wg, wu, wd, g_out, gate, tm, th):
    b, s, d = x.shape
    ne, _, hid = wg.shape
    tm = min(tm, s)
    router_p = jnp.zeros((d, LANES), F32).at[:, :ne].set(router)
    row = pl.BlockSpec((None, tm, d), lambda i, m, e, j: (i, m, 0))
    vec = pl.BlockSpec((1, d), lambda i, m, e, j: (0, 0))
    bvec = pl.BlockSpec((None, 1, d), lambda i, m, e, j: (i, 0, 0))
    return pl.pallas_call(
        _moe_kernel,
        out_shape=jax.ShapeDtypeStruct((b, s, d), F32),
        grid=(b, s // tm, ne, hid // th),
        in_specs=[
            row, vec, bvec, bvec,
            pl.BlockSpec((d, LANES), lambda i, m, e, j: (0, 0)),
            pl.BlockSpec((None, d, th), lambda i, m, e, j: (e, 0, j)),
            pl.BlockSpec((None, d, th), lambda i, m, e, j: (e, 0, j)),
            pl.BlockSpec((None, th, d), lambda i, m, e, j: (e, j, 0)),
            vec, bvec,
        ],
        out_specs=row,
        scratch_shapes=[pltpu.VMEM((tm, d), BF16), pltpu.VMEM((tm, LANES), F32),
                        pltpu.VMEM((tm, d), F32)],
        compiler_params=_params("parallel", "parallel", "arbitrary", "arbitrary"),
        name="moe",
    )(x, g_in.reshape(1, d), shift.reshape(b, 1, d), scale.reshape(b, 1, d), router_p, wg, wu, wd,
      g_out.reshape(1, d), gate.reshape(b, 1, d))


def _rope_tables(rows, dim):
    row = jnp.repeat(jnp.arange(rows, dtype=F32), GRID_W)
    col = jnp.tile(jnp.arange(GRID_W, dtype=F32), rows)
    n_freq = dim // 4
    freqs = ROPE_THETA ** (-jnp.arange(n_freq, dtype=F32) / n_freq)
    ang = jnp.concatenate([row[:, None] * freqs, col[:, None] * freqs], axis=-1)
    return jnp.cos(ang), jnp.sin(ang)


def _rope(x, cs):
    cos, sin = cs
    half = x.shape[-1] // 2
    x1, x2 = x[..., :half], x[..., half:]
    return jnp.concatenate([x1 * cos - x2 * sin, x1 * sin + x2 * cos], axis=-1)


def _heads(t, nh, d):
    b, n, _ = t.shape
    return t.reshape(b, n, nh, d).transpose(0, 2, 1, 3)


def _unheads(t):
    b, nh, n, d = t.shape
    return t.transpose(0, 2, 1, 3).reshape(b, n, nh * d)


def _even_mixer(hc_p, hl_p, rope_ret, rope_gqa, decay_exp, ret_g, q_g, k_g, need_ctx):
    n_ctx = hc_p.shape[1]
    log_gamma = jnp.log1p(-jnp.exp2(-decay_exp.astype(F32)))
    rw = RET_HEADS * RET_DK
    gw = GQA_Q_HEADS * GQA_HEAD_DIM
    kw = GQA_KV_HEADS * GQA_HEAD_DIM

    def split(p, rope_r, rope_a):
        rq = _heads(p[..., 0:rw], RET_HEADS, RET_DK)
        rk = _heads(p[..., rw:2 * rw], RET_HEADS, RET_DK) * RET_DK ** -0.5
        rv = _heads(p[..., 2 * rw:3 * rw], RET_HEADS, RET_DV)
        rg = p[..., 3 * rw:4 * rw]
        o = 4 * rw
        aq = _heads(p[..., o:o + gw], GQA_Q_HEADS, GQA_HEAD_DIM)
        ak = _heads(p[..., o + gw:o + gw + kw], GQA_KV_HEADS, GQA_HEAD_DIM)
        av = _heads(p[..., o + gw + kw:o + gw + 2 * kw], GQA_KV_HEADS, GQA_HEAD_DIM)
        aq = _rms(aq) * q_g
        ak = _rms(ak) * k_g
        if rope_r is not None:
            rq, rk = _rope(rq, rope_r), _rope(rk, rope_r)
            aq, ak = _rope(aq, rope_a), _rope(ak, rope_a)
        return (rq, rk, rv), rg, (aq * GQA_HEAD_DIM ** -0.5, ak, av)

    (crq, crk, crv), c_gate, (cq, ck, cv) = split(hc_p, None, None)
    (lrq, lrk, lrv), l_gate, (lq, lk, lv) = split(hl_p, rope_ret, rope_gqa)

    cat = lambda a, b_: jnp.concatenate([a, b_], axis=2)
    y_ret = _retention(cat(crq, lrq), cat(crk, lrk), cat(crv, lrv), log_gamma, n_ctx)
    y_ret = y_ret[0] + y_ret[1]

    def ret_out(y, gate):
        y = _rms(y.transpose(0, 2, 1, 3)) * ret_g.reshape(RET_HEADS, RET_DV)
        return y.reshape(gate.shape) * _silu(gate)

    group = GQA_Q_HEADS // GQA_KV_HEADS
    k_all = cat(ck, lk).astype(BF16)
    v_all = cat(cv, lv).astype(BF16)
    attn_lat = _attention(lq.astype(BF16), k_all, v_all, group, group, tq=256)
    mix_lat = jnp.concatenate([ret_out(y_ret[:, :, n_ctx:], l_gate), _unheads(attn_lat)], axis=-1)
    mix_ctx = None
    if need_ctx:
        attn_ctx = _attention(cq.astype(BF16), ck.astype(BF16), cv.astype(BF16), group, group, tq=256)
        mix_ctx = jnp.concatenate([ret_out(y_ret[:, :, :n_ctx], c_gate), _unheads(attn_ctx)], axis=-1)
    return mix_ctx, mix_lat


def _token_shift(p, mu):
    prev = jnp.pad(p[:, :-1], ((0, 0), (1, 0), (0, 0)))
    nxt = jnp.pad(p[:, 1:], ((0, 0), (0, 1), (0, 0)))
    return p + mu * (0.5 * (prev + nxt) - p)


def _odd_mixer(hc_p, hl_p, rope_diff, lam_vec, subln_g, lam_init, mu, w0, w2, a0, a2, g2, k_k, k_a,
               r_k, ln_g, ln_b, need_ctx):
    b = hl_p.shape[0]
    n_ctx = hc_p.shape[1]
    lamv = lam_vec.astype(F32)
    lam = jnp.exp(jnp.sum(lamv[0] * lamv[1])) - jnp.exp(jnp.sum(lamv[2] * lamv[3])) + lam_init
    dw = DIFF_HEADS * 2 * DIFF_HEAD_DIM

    def split(p, rope):
        q = _heads(p[..., 0:dw], 2 * DIFF_HEADS, DIFF_HEAD_DIM)
        k = _heads(p[..., dw:2 * dw], 2 * DIFF_HEADS, DIFF_HEAD_DIM)
        v = _heads(p[..., 2 * dw:3 * dw], DIFF_HEADS, DIFF_V_DIM)
        if rope is not None:
            q, k = _rope(q, rope), _rope(k, rope)
        return (q * DIFF_HEAD_DIM ** -0.5, k, v), _token_shift(p[..., 3 * dw:], mu)

    (cq, ck, cv), c_rw = split(hc_p, None)
    (lq, lk, lv), l_rw = split(hl_p, rope_diff)

    p_all = jnp.concatenate([c_rw, l_rw], axis=1)
    pad_rows = lambda m, d: jnp.zeros((2 * m.shape[0], m.shape[1]), F32).at[
        d * m.shape[0]:(d + 1) * m.shape[0]].set(m)
    w2p = jnp.stack([pad_rows(w2[d], d) for d in range(2)]).astype(BF16)
    a2p = jnp.stack([pad_rows(a2[d], d) for d in range(2)]).astype(BF16)
    y_rw = _rwkv_scan(p_all, w0, w2p, a0, a2p, k_k, k_a, n_ctx)
    y_rw = y_rw[:, 0] + y_rw[:, 1]

    def finish(y, p):
        n = p.shape[1]
        hd = lambda t: t.reshape(b, n, RWKV_HEADS, RWKV_HEAD_DIM)
        r, k, v = hd(p[..., 0:RWKV_W]), hd(p[..., RWKV_W:2 * RWKV_W]), hd(p[..., 2 * RWKV_W:3 * RWKV_W])
        gd = p[..., 3 * RWKV_W + 2 * RWKV_W_LORA + 2 * RWKV_A_LORA:]
        y = hd(y)
        mu_ = jnp.mean(y, axis=-1, keepdims=True)
        var = jnp.mean(jnp.square(y - mu_), axis=-1, keepdims=True)
        y = ((y - mu_) * lax.rsqrt(var + RWKV_LN_EPS) * ln_g.reshape(RWKV_HEADS, RWKV_HEAD_DIM)
             + ln_b.reshape(RWKV_HEADS, RWKV_HEAD_DIM))
        bonus = jnp.sum(r * k * r_k, axis=-1, keepdims=True) * v
        return (y + bonus).reshape(b, n, RWKV_W), _sigmoid(gd)

    cat = lambda a_, b_: jnp.concatenate([a_, b_], axis=2)

    def diff_out(o):
        o = o.reshape(b, DIFF_HEADS, 2, o.shape[2], DIFF_V_DIM)
        o = o[:, :, 0] - lam * o[:, :, 1]
        o = _rms(o.transpose(0, 2, 1, 3)) * subln_g * (1.0 - lam_init)
        return o.reshape(b, o.shape[1], DIFF_HEADS * DIFF_V_DIM)

    attn_lat = _attention(lq.astype(BF16), cat(ck, lk).astype(BF16), cat(cv, lv).astype(BF16), 1, 2, tq=256)
    rw_lat = finish(y_rw[:, n_ctx:], l_rw)
    out_lat = (diff_out(attn_lat), rw_lat)
    out_ctx = None
    if need_ctx:
        attn_ctx = _attention(cq.astype(BF16), ck.astype(BF16), cv.astype(BF16), 1, 2, tq=256)
        out_ctx = (diff_out(attn_ctx), finish(y_rw[:, :n_ctx], c_rw))
    return out_ctx, out_lat


def _gate_kernel(y_ref, s_ref, g2_ref, o_ref):
    o_ref[...] = (y_ref[...] * _mm(s_ref[...].astype(BF16), g2_ref[...])).astype(o_ref.dtype)


def _rwkv_gate(y, sg, g2, tm):
    b, n, w = y.shape
    tm = min(tm, n)
    return pl.pallas_call(
        _gate_kernel,
        out_shape=jax.ShapeDtypeStruct((b, n, w), F32),
        grid=(b, n // tm),
        in_specs=[
            pl.BlockSpec((None, tm, w), lambda i, j: (i, j, 0)),
            pl.BlockSpec((None, tm, sg.shape[2]), lambda i, j: (i, j, 0)),
            pl.BlockSpec(g2.shape, lambda i, j: (0, 0)),
        ],
        out_specs=pl.BlockSpec((None, tm, w), lambda i, j: (i, j, 0)),
        compiler_params=_params("parallel", "parallel"),
        name="rwkv_gate",
    )(y, sg, g2.astype(BF16))


def kernel(x, c, ctx, c_ctx, mod_w, mod_b, norm_g, even_w_in, even_w_out, ret_decay_exp, ret_norm_g, gqa_q_norm, gqa_k_norm, ffn_w_gate, ffn_w_up, ffn_w_down, odd_w_in, odd_w_out, diff_lambda, diff_subln_g, rwkv_mu, rwkv_w0, rwkv_w2, rwkv_a0, rwkv_a2, rwkv_g2, rwkv_k_k, rwkv_k_a, rwkv_r_k, rwkv_ln_g, rwkv_ln_b, moe_router, moe_w_gate, moe_w_up, moe_w_down):
    bsz, seq, d = x.shape
    depth = mod_w.shape[0]
    rows = seq // GRID_W
    rope_ret = _rope_tables(rows, RET_DK)
    rope_gqa = _rope_tables(rows, GQA_HEAD_DIM)
    rope_diff = _rope_tables(rows, DIFF_HEAD_DIM)

    cond_rows = 8 * ((bsz + 1 + 7) // 8)
    cond = jnp.zeros((cond_rows, d), F32).at[:bsz].set(c).at[bsz].set(c_ctx)
    mods = _modulation(cond, mod_w, mod_b)

    for layer in range(depth):
        i = layer // 2
        need_ctx = layer < depth - 1
        lat_mod = [mods[layer, :bsz, m * d:(m + 1) * d] for m in range(6)]
        ctx_mod = [jnp.broadcast_to(mods[layer, bsz, m * d:(m + 1) * d], (bsz, d)) for m in range(6)]
        l_sh1, l_sc1, l_g1, l_sh2, l_sc2, l_g2 = lat_mod
        c_sh1, c_sc1, c_g1, c_sh2, c_sc2, c_g2 = ctx_mod
        g = norm_g[layer]

        if layer % 2 == 0:
            w_in = even_w_in[i].astype(BF16)
            w_out = even_w_out[i].astype(BF16)
        else:
            w_in = odd_w_in[i].astype(BF16)
            w_out = odd_w_out[i].astype(BF16)
        hl_p = _proj_in(x, g[0], l_sh1, l_sc1, w_in, tm=256)
        hc_p = _proj_in(ctx, g[0], c_sh1, c_sc1, w_in, tm=256)

        if layer % 2 == 0:
            mix_ctx, mix_lat = _even_mixer(hc_p, hl_p, rope_ret, rope_gqa, ret_decay_exp[i], ret_norm_g[i],
                                           gqa_q_norm[i], gqa_k_norm[i], need_ctx)
        else:
            lam_init = 0.8 - 0.6 * math.exp(-0.3 * layer)
            out_ctx, out_lat = _odd_mixer(hc_p, hl_p, rope_diff, diff_lambda[i], diff_subln_g[i], lam_init,
                                          rwkv_mu[i], rwkv_w0[i], rwkv_w2[i], rwkv_a0[i], rwkv_a2[i],
                                          rwkv_g2[i], rwkv_k_k[i], rwkv_k_a[i], rwkv_r_k[i], rwkv_ln_g[i],
                                          rwkv_ln_b[i], need_ctx)

            def join(parts):
                diff_part, (yb, sg) = parts
                return jnp.concatenate([diff_part, _rwkv_gate(yb, sg, rwkv_g2[i], tm=512)], axis=-1)

            mix_lat = join(out_lat)
            mix_ctx = join(out_ctx) if need_ctx else None

        x = _proj_out(mix_lat.astype(BF16), w_out, x, g[1], l_g1, tm=512)
        if need_ctx:
            ctx = _proj_out(mix_ctx.astype(BF16), w_out, ctx, g[1], c_g1, tm=512)

        if layer % 2 == 0:
            wg, wu, wd = (ffn_w_gate[i].astype(BF16), ffn_w_up[i].astype(BF16), ffn_w_down[i].astype(BF16))
            th = _hidden_tile(wg.shape[1])
            x = _ffn(x, g[2], l_sh2, l_sc2, wg, wu, wd, g[3], l_g2, tm=1024, th=th)
            if need_ctx:
                ctx = _ffn(ctx, g[2], c_sh2, c_sc2, wg, wu, wd, g[3], c_g2, tm=1024, th=th)
        else:
            wg, wu, wd = (moe_w_gate[i].astype(BF16), moe_w_up[i].astype(BF16), moe_w_down[i].astype(BF16))
            th = _hidden_tile(wg.shape[2])
            x = _moe(x, g[2], l_sh2, l_sc2, moe_router[i], wg, wu, wd, g[3], l_g2, tm=1024, th=th)
            if need_ctx:
                ctx = _moe(ctx, g[2], c_sh2, c_sc2, moe_router[i], wg, wu, wd, g[3], c_g2, tm=1024, th=th)
    return x


def _hidden_tile(hidden, target=512):
    best = LANES
    for t in range(LANES, target + 1, LANES):
        if hidden % t == 0:
            best = t
    return best
```
